```python
import math
import jax, jax.numpy as jnp
from jax import lax
import numpy as np

D_MODEL = 2048
BATCH = 4
SEQ = 2048
DEPTH = 2

HEAD_DIM = 128
N_MIXERS = 2
DIL_GROUPS = ((128, 1), (512, 4), (2048, 16))
N_DIL_GROUPS = len(DIL_GROUPS)
DIL_HEADS = D_MODEL // HEAD_DIM
DIL_IN = (2 * N_DIL_GROUPS + 1) * D_MODEL
DIFF_HEADS = D_MODEL // (2 * HEAD_DIM)
D_FF = ((8 * D_MODEL // 3 + 255) // 256) * 256
ROPE_THETA = 10000.0
NORM_EPS = 1e-6
Q_BLOCK = 128
N_DIL_LAYERS = (DEPTH + 1) // 2
N_DIFF_LAYERS = DEPTH // 2

kernel_name = "hybrid_dilated_diff_macaron"


def rms_norm(x, g, eps=NORM_EPS):
    xf = x.astype(jnp.float32)
    y = xf * lax.rsqrt(jnp.mean(xf * xf, axis=-1, keepdims=True) + eps)
    return (y * g.astype(jnp.float32)).astype(x.dtype)


def rope_tables(seq):
    inv_freq = ROPE_THETA ** (-jnp.arange(0, HEAD_DIM, 2, dtype=jnp.float32) / HEAD_DIM)
    ang = jnp.arange(seq, dtype=jnp.float32)[:, None] * inv_freq[None, :]
    return jnp.cos(ang), jnp.sin(ang)


def apply_rope(x, cos, sin):
    shape = (x.shape[1],) + (1,) * (x.ndim - 3) + (cos.shape[-1],)
    c = cos.reshape(shape)
    s = sin.reshape(shape)
    xf = x.astype(jnp.float32)
    x1, x2 = jnp.split(xf, 2, axis=-1)
    return jnp.concatenate([x1 * c - x2 * s, x2 * c + x1 * s], axis=-1).astype(x.dtype)


def swiglu(h, w_gate_up, w_down):
    g, u = jnp.split(h @ w_gate_up, 2, axis=-1)
    return (jax.nn.silu(g) * u) @ w_down


def dilated_window_attention(q, k, v, window, dilation):
    B, S, H, Dh = q.shape
    d = dilation
    C = window // dilation
    span = C * d
    S_pad = -(-S // span) * span
    NB = S_pad // span
    pad = S_pad - S

    def to_blocks(t):
        t = jnp.pad(t, ((0, 0), (0, pad), (0, 0), (0, 0)))
        return t.reshape(B, NB, C, d, H, t.shape[-1])

    def with_prev(t):
        prev = jnp.pad(t, ((0, 0), (1, 0), (0, 0), (0, 0), (0, 0), (0, 0)))[:, :-1]
        return jnp.concatenate([prev, t], axis=2)

    qb = to_blocks(q) * (Dh ** -0.5)
    kk = with_prev(to_blocks(k))
    vv = with_prev(to_blocks(v))
    s = jnp.einsum('bnqrhd,bnkrhd->bnrhqk', qb, kk).astype(jnp.float32)
    qi = jnp.arange(C)[:, None]
    ki = jnp.arange(2 * C)[None, :]
    band = (ki >= qi) & (ki <= qi + C)
    blk = jnp.arange(NB)[:, None, None]
    valid = band[None] & ((blk > 0) | (ki[None] >= C))
    s = jnp.where(valid[None, :, None, None], s, -jnp.inf)
    lse = jax.nn.logsumexp(s, axis=-1)
    p = jnp.exp(s - lse[..., None])
    o = jnp.einsum('bnrhqk,bnkrhe->bnqrhe', p.astype(v.dtype), vv)
    o = o.reshape(B, S_pad, H, Dh)[:, :S]
    lse = jnp.transpose(lse, (0, 1, 4, 2, 3)).reshape(B, S_pad, H)[:, :S]
    return o, lse


def dilated_mixer(h, w_in, w_out, cos, sin):
    B, S, _ = h.shape
    proj = h @ w_in
    gd = N_DIL_GROUPS * D_MODEL
    q = apply_rope(proj[..., :gd].reshape(B, S, N_DIL_GROUPS, DIL_HEADS, HEAD_DIM), cos, sin)
    k = apply_rope(proj[..., gd:2 * gd].reshape(B, S, N_DIL_GROUPS, DIL_HEADS, HEAD_DIM), cos, sin)
    v = proj[..., 2 * gd:].reshape(B, S, DIL_HEADS, HEAD_DIM)
    outs, lses = [], []
    for g, (window, dilation) in enumerate(DIL_GROUPS):
        o, l = dilated_window_attention(q[:, :, g], k[:, :, g], v, window, dilation)
        outs.append(o)
        lses.append(l)
    wts = jax.nn.softmax(jnp.stack(lses, axis=0), axis=0)
    o = jnp.einsum('gbsh,gbshd->bshd', wts, jnp.stack(outs, axis=0).astype(jnp.float32))
    return o.astype(h.dtype).reshape(B, S, D_MODEL) @ w_out


def diff_mixer(h, w_in, w_out, lam_params, subln_gain, lam_init, cos, sin):
    B, S, _ = h.shape
    q, k, v = jnp.split(h @ w_in, 3, axis=-1)
    q = apply_rope(q.reshape(B, S, DIFF_HEADS, 2, HEAD_DIM), cos, sin) * (HEAD_DIM ** -0.5)
    k = apply_rope(k.reshape(B, S, DIFF_HEADS, 2, HEAD_DIM), cos, sin)
    v = v.reshape(B, S, DIFF_HEADS, 2 * HEAD_DIM)
    lp = lam_params.astype(jnp.float32)
    lam = jnp.exp(jnp.sum(lp[0] * lp[1])) - jnp.exp(jnp.sum(lp[2] * lp[3])) + lam_init
    nb = S // Q_BLOCK
    qb = jnp.moveaxis(q, 3, 0).reshape(2, B, nb, Q_BLOCK, DIFF_HEADS, HEAD_DIM)
    qb = jnp.moveaxis(qb, 2, 0)
    kk = jnp.moveaxis(k, 3, 0)
    kpos = jnp.arange(S)

    def block(args):
        qblk, start = args
        s = jnp.einsum('cbqhd,cbkhd->cbhqk', qblk, kk).astype(jnp.float32)
        qpos = start + jnp.arange(Q_BLOCK)
        s = jnp.where(qpos[:, None] >= kpos[None, :], s, -jnp.inf)
        p = jax.nn.softmax(s, axis=-1)
        a = p[0] - lam * p[1]
        return jnp.einsum('bhqk,bkhe->bqhe', a.astype(v.dtype), v)

    o = lax.map(block, (qb, jnp.arange(nb) * Q_BLOCK))
    o = jnp.moveaxis(o, 0, 1).reshape(B, S, DIFF_HEADS, 2 * HEAD_DIM)
    o = rms_norm(o, subln_gain, eps=1e-5) * (1.0 - lam_init)
    return o.reshape(B, S, D_MODEL) @ w_out


def setup_inputs(seed: int = 0) -> dict:
    key = jax.random.key(seed)
    ks = jax.random.split(key, 12)
    f32 = jnp.float32
    D = D_MODEL
    x = jax.random.normal(ks[0], (BATCH, SEQ, D), f32)
    norms = 1.0 + 0.05 * jax.random.normal(ks[1], (DEPTH, 6, D), f32)
    ffn_w_gate_up = jax.random.normal(ks[2], (DEPTH, 2, D, 2 * D_FF), f32) * D ** -0.5
    ffn_w_down = jax.random.normal(ks[3], (DEPTH, 2, D_FF, D), f32) * D_FF ** -0.5
    dil_w_in = jax.random.normal(ks[4], (N_DIL_LAYERS, D, DIL_IN), f32) * D ** -0.5
    dil_w_out = jax.random.normal(ks[5], (N_DIL_LAYERS, D, D), f32) * D ** -0.5
    diff_w_in = jax.random.normal(ks[6], (N_DIFF_LAYERS, D, 3 * D), f32) * D ** -0.5
    diff_w_out = jax.random.normal(ks[7], (N_DIFF_LAYERS, D, D), f32) * D ** -0.5
    diff_lambda = 0.1 * jax.random.normal(ks[8], (N_DIFF_LAYERS, 4, HEAD_DIM), f32)
    diff_subln = 1.0 + 0.05 * jax.random.normal(ks[9], (N_DIFF_LAYERS, 2 * HEAD_DIM), f32)
    return {"x": x, "norms": norms, "ffn_w_gate_up": ffn_w_gate_up, "ffn_w_down": ffn_w_down,
            "dil_w_in": dil_w_in, "dil_w_out": dil_w_out, "diff_w_in": diff_w_in,
            "diff_w_out": diff_w_out, "diff_lambda": diff_lambda, "diff_subln": diff_subln}


def reference(x, norms, ffn_w_gate_up, ffn_w_down, dil_w_in, dil_w_out, diff_w_in,
              diff_w_out, diff_lambda, diff_subln):
    cos, sin = rope_tables(x.shape[1])
    for i in range(DEPTH):
        nm = norms[i]
        h = rms_norm(x, nm[0])
        x = x + 0.5 * rms_norm(swiglu(h, ffn_w_gate_up[i, 0], ffn_w_down[i, 0]), nm[1])
        h = rms_norm(x, nm[2])
        j = i // N_MIXERS
        if i % N_MIXERS == 0:
            m = dilated_mixer(h, dil_w_in[j], dil_w_out[j], cos, sin)
        else:
            lam_init = 0.8 - 0.6 * math.exp(-0.3 * i)
            m = diff_mixer(h, diff_w_in[j], diff_w_out[j], diff_lambda[j], diff_subln[j],
                           lam_init, cos, sin)
        x = x + rms_norm(m, nm[3])
        h = rms_norm(x, nm[4])
        x = x + 0.5 * rms_norm(swiglu(h, ffn_w_gate_up[i, 1], ffn_w_down[i, 1]), nm[5])
    return x
```

```python
import functools
import math

import jax
import jax.numpy as jnp
from jax import lax
from jax.experimental import pallas as pl
from jax.experimental.pallas import tpu as pltpu

D_MODEL = 2048
HEAD_DIM = 128
DIL_GROUPS = ((128, 1), (512, 4), (2048, 16))
DILATIONS = tuple(d for _, d in DIL_GROUPS)
BAND = 128
DIL_HEADS = D_MODEL // HEAD_DIM
DIFF_HEADS = D_MODEL // (2 * HEAD_DIM)
D_FF = 5632
ROPE_THETA = 10000.0
NORM_EPS = 1e-6
SUBLN_EPS = 1e-5
N_MIXERS = 2

V7X_VMEM_LIMIT_BYTES = 56 * 1024 * 1024

F32 = jnp.float32
BF16 = jnp.bfloat16


def _params(semantics):
    return pltpu.CompilerParams(dimension_semantics=semantics,
                                vmem_limit_bytes=V7X_VMEM_LIMIT_BYTES)


def _rms(x, gain, eps):
    ms = jnp.mean(x * x, axis=-1, keepdims=True)
    return x * lax.rsqrt(ms + eps) * gain


def _ffn_up_kernel(x_ref, g_ref, wg_ref, wu_ref, o_ref, h_ref):
    @pl.when(pl.program_id(1) == 0)
    def _():
        h_ref[...] = _rms(x_ref[...], g_ref[...], NORM_EPS).astype(BF16)

    h = h_ref[...]
    g = jnp.dot(h, wg_ref[...], preferred_element_type=F32)
    u = jnp.dot(h, wu_ref[...], preferred_element_type=F32)
    o_ref[...] = (g * jax.nn.sigmoid(g) * u).astype(BF16)


def _ffn_up(x, gain, w_gate_up, layer, half, tm=1024, tn=512):
    T, D = x.shape
    n_tiles = D_FF // tn
    return pl.pallas_call(
        _ffn_up_kernel,
        grid=(T // tm, n_tiles),
        in_specs=[
            pl.BlockSpec((tm, D), lambda i, j: (i, 0)),
            pl.BlockSpec((1, D), lambda i, j: (0, 0)),
            pl.BlockSpec((None, None, D, tn), lambda i, j: (layer, half, 0, j)),
            pl.BlockSpec((None, None, D, tn), lambda i, j: (layer, half, 0, j + n_tiles)),
        ],
        out_specs=pl.BlockSpec((tm, tn), lambda i, j: (i, j)),
        out_shape=jax.ShapeDtypeStruct((T, D_FF), BF16),
        scratch_shapes=[pltpu.VMEM((tm, D), BF16)],
        compiler_params=_params(("parallel", "arbitrary")),
        name="ffn_up",
    )(x, gain, w_gate_up, w_gate_up)


def _mm_norm_res_kernel(a_ref, w_ref, x_ref, g_ref, o_ref, *, n_k, res_scale):
    k = pl.program_id(1)
    part = jnp.dot(a_ref[...], w_ref[...], preferred_element_type=F32)

    @pl.when(k == 0)
    def _():
        o_ref[...] = part

    @pl.when(k > 0)
    def _():
        o_ref[...] += part

    @pl.when(k == n_k - 1)
    def _():
        o_ref[...] = x_ref[...] + res_scale * _rms(o_ref[...], g_ref[...], NORM_EPS)


def _mm_norm_res(a, w, w_index, x, gain, res_scale, name, tm=1024, tk=512):
    T, K = a.shape
    D = x.shape[1]
    n_k = K // tk
    lead = (None,) * len(w_index)
    return pl.pallas_call(
        functools.partial(_mm_norm_res_kernel, n_k=n_k, res_scale=res_scale),
        grid=(T // tm, n_k),
        in_specs=[
            pl.BlockSpec((tm, tk), lambda i, k: (i, k)),
            pl.BlockSpec(lead + (tk, D), lambda i, k: w_index + (k, 0)),
            pl.BlockSpec((tm, D), lambda i, k: (i, 0)),
            pl.BlockSpec((1, D), lambda i, k: (0, 0)),
        ],
        out_specs=pl.BlockSpec((tm, D), lambda i, k: (i, 0)),
        out_shape=jax.ShapeDtypeStruct((T, D), F32),
        compiler_params=_params(("parallel", "arbitrary")),
        name=name,
    )(a, w, x, gain)


def _rope(y, cos, sin_signed):
    return y * cos + pltpu.roll(y, HEAD_DIM // 2, axis=1) * sin_signed


def _proj_in_kernel(*refs, segments, tn, seq_tiles):
    n_seg = len(segments)
    x_ref, g_ref, cos_ref, sin_ref = refs[:4]
    w_refs = refs[4:4 + n_seg]
    n_out = sum(len(seg[2]) for seg in segments)
    out_refs = refs[4 + n_seg:4 + n_seg + n_out]
    h_ref, y_ref = refs[4 + n_seg + n_out:]

    @pl.when(pl.program_id(1) == 0)
    def _():
        h_ref[...] = _rms(x_ref[...], g_ref[...], NORM_EPS).astype(BF16)

    h = h_ref[...]
    tm = h.shape[0]
    o = 0
    for s, (use_rope, scale, dils) in enumerate(segments):
        y = jnp.dot(h, w_refs[s][...], preferred_element_type=F32)
        heads = [y[:, c * HEAD_DIM:(c + 1) * HEAD_DIM] for c in range(tn // HEAD_DIM)]
        if use_rope:
            cos = cos_ref[...]
            sin = sin_ref[...]
            heads = [_rope(yh, cos, sin) * scale for yh in heads]
        if any(d > 1 for d in dils):
            for c, yh in enumerate(heads):
                y_ref[c] = yh
        for d in dils:
            out_ref = out_refs[o]
            o += 1
            for c, yh in enumerate(heads):
                cols = slice(c * HEAD_DIM, (c + 1) * HEAD_DIM)
                if d == 1:
                    out_ref[0, :, cols] = yh.astype(BF16)
                else:
                    for r in range(d):
                        out_ref[r, :, cols] = y_ref[c, pl.ds(r, tm // d, stride=d), :].astype(BF16)


def _proj_in(x, gain, w, w_layer, cos, sin_signed, segments, batch, seq, tm=1024, tn=256):
    T, D = x.shape
    seq_tiles = seq // tm
    n_col = D_MODEL // tn
    in_specs = [
        pl.BlockSpec((tm, D), lambda i, j: (i, 0)),
        pl.BlockSpec((1, D), lambda i, j: (0, 0)),
        pl.BlockSpec((tm, HEAD_DIM), lambda i, j: (i % seq_tiles, 0)),
        pl.BlockSpec((tm, HEAD_DIM), lambda i, j: (i % seq_tiles, 0)),
    ]
    out_specs, out_shapes = [], []
    for off, _, _, dils in segments:
        in_specs.append(pl.BlockSpec((None, D, tn),
                                     functools.partial(lambda i, j, c: (w_layer, 0, c + j), c=off // tn)))
        for d in dils:
            out_specs.append(pl.BlockSpec((None, d, tm // d, tn),
                                          lambda i, j: (i // seq_tiles, 0, i % seq_tiles, j)))
            out_shapes.append(jax.ShapeDtypeStruct((batch, d, seq // d, D_MODEL), BF16))
    kern = functools.partial(_proj_in_kernel, segments=tuple(s[1:] for s in segments), tn=tn,
                             seq_tiles=seq_tiles)
    return pl.pallas_call(
        kern,
        grid=(T // tm, n_col),
        in_specs=in_specs,
        out_specs=out_specs,
        out_shape=out_shapes,
        scratch_shapes=[pltpu.VMEM((tm, D), BF16), pltpu.VMEM((tn // HEAD_DIM, tm, HEAD_DIM), F32)],
        compiler_params=_params(("parallel", "arbitrary")),
        name="proj_in",
    )(x, gain, cos, sin_signed, *([w] * len(segments)))


def _band_block(q, k, v, delta):
    s = lax.dot_general(q, k, (((1,), (1,)), ((), ())), preferred_element_type=F32)
    back = delta + lax.broadcasted_iota(jnp.int32, s.shape, 0) - lax.broadcasted_iota(jnp.int32, s.shape, 1)
    s = jnp.where((back >= 0) & (back <= BAND), s, -jnp.inf)
    m = jnp.max(s, axis=-1, keepdims=True)
    p = jnp.exp(s - m)
    l = jnp.sum(p, axis=-1, keepdims=True)
    o = jnp.dot(p.astype(BF16), v, preferred_element_type=F32) / l
    return o, m + jnp.log(l)


def _dil_attn_kernel(q0, k0, v0, q1, k1, v1, q2, k2, v2, out_ref, o_scr, lse_scr, *, seq):
    groups = ((q0, k0, v0), (q1, k1, v1), (q2, k2, v2))
    for g, ((q_ref, k_ref, v_ref), d) in enumerate(zip(groups, DILATIONS)):
        n_blocks = seq // d // BAND
        for r in range(d):
            if n_blocks == 1:
                o, lse = _band_block(q_ref[r], k_ref[r], v_ref[r], 0)
                o_scr[g, pl.ds(r, BAND, stride=d), :] = o
                lse_scr[g, pl.ds(r, BAND, stride=d), :] = jnp.broadcast_to(lse, (BAND, HEAD_DIM))
                continue

            def body(n, carry, q_ref=q_ref, k_ref=k_ref, v_ref=v_ref, r=r, d=d, g=g):
                k_start = pl.multiple_of(jnp.maximum(n - 1, 0) * BAND, BAND)
                q_start = pl.multiple_of(n * BAND, BAND)
                q = q_ref[r, pl.ds(q_start, BAND), :]
                k = k_ref[r, pl.ds(k_start, 2 * BAND), :]
                v = v_ref[r, pl.ds(k_start, 2 * BAND), :]
                o, lse = _band_block(q, k, v, q_start - k_start)
                tok = q_start * d + r
                if d == 1:
                    o_scr[g, pl.ds(q_start, BAND), :] = o
                    lse_scr[g, pl.ds(q_start, BAND), :] = jnp.broadcast_to(lse, (BAND, HEAD_DIM))
                else:
                    o_scr[g, pl.ds(tok, BAND, stride=d), :] = o
                    lse_scr[g, pl.ds(tok, BAND, stride=d), :] = jnp.broadcast_to(lse, (BAND, HEAD_DIM))
                return carry

            lax.fori_loop(0, n_blocks, body, 0)

    l0, l1, l2 = lse_scr[0], lse_scr[1], lse_scr[2]
    m = jnp.maximum(jnp.maximum(l0, l1), l2)
    e0, e1, e2 = jnp.exp(l0 - m), jnp.exp(l1 - m), jnp.exp(l2 - m)
    den = e0 + e1 + e2
    out = (e0 / den) * o_scr[0] + (e1 / den) * o_scr[1] + (e2 / den) * o_scr[2]
    out_ref[...] = out.astype(BF16)


def _dil_attn(qkv, batch, seq):
    in_specs = []
    for d in DILATIONS:
        for _ in range(3):
            in_specs.append(pl.BlockSpec((None, d, seq // d, HEAD_DIM), lambda b, h: (b, 0, 0, h)))
    return pl.pallas_call(
        functools.partial(_dil_attn_kernel, seq=seq),
        grid=(batch, DIL_HEADS),
        in_specs=in_specs,
        out_specs=pl.BlockSpec((seq, HEAD_DIM), lambda b, h: (b, h)),
        out_shape=jax.ShapeDtypeStruct((batch * seq, D_MODEL), BF16),
        scratch_shapes=[pltpu.VMEM((3, seq, HEAD_DIM), F32), pltpu.VMEM((3, seq, HEAD_DIM), F32)],
        compiler_params=_params(("parallel", "parallel")),
        name="dil_attn",
    )(*qkv)


def _diff_attn_kernel(q_ref, k_ref, v_ref, lam_ref, gain_ref, o_ref, m_scr, l_scr, acc_scr, *,
                      tq, lam_init):
    qi = pl.program_id(2)
    m_scr[...] = jnp.full(m_scr.shape, -jnp.inf, F32)
    l_scr[...] = jnp.zeros(l_scr.shape, F32)
    acc_scr[...] = jnp.zeros(acc_scr.shape, F32)

    def step(j, masked):
        k_start = pl.multiple_of(j * tq, tq)
        v = v_ref[pl.ds(k_start, tq), :]
        for c in range(2):
            q = q_ref[:, c * HEAD_DIM:(c + 1) * HEAD_DIM]
            k = k_ref[pl.ds(k_start, tq), c * HEAD_DIM:(c + 1) * HEAD_DIM]
            s = lax.dot_general(q, k, (((1,), (1,)), ((), ())), preferred_element_type=F32)
            if masked:
                row = lax.broadcasted_iota(jnp.int32, s.shape, 0)
                col = lax.broadcasted_iota(jnp.int32, s.shape, 1)
                s = jnp.where(row >= col, s, -jnp.inf)
            m_old = m_scr[c]
            m_new = jnp.maximum(m_old, jnp.max(s, axis=-1, keepdims=True))
            alpha = jnp.exp(m_old - m_new)
            p = jnp.exp(s - m_new)
            l_scr[c] = alpha * l_scr[c] + jnp.sum(p, axis=-1, keepdims=True)
            acc_scr[c] = alpha * acc_scr[c] + jnp.dot(p.astype(BF16), v, preferred_element_type=F32)
            m_scr[c] = m_new

    def body(j, carry):
        step(j, False)
        return carry

    lax.fori_loop(0, qi, body, 0)
    step(qi, True)

    lp = lam_ref[...]
    lam = (jnp.exp(jnp.sum(lp[0:1] * lp[1:2], axis=-1, keepdims=True))
           - jnp.exp(jnp.sum(lp[2:3] * lp[3:4], axis=-1, keepdims=True)) + lam_init)
    o = acc_scr[0] / l_scr[0] - lam * (acc_scr[1] / l_scr[1])
    o_ref[...] = (_rms(o, gain_ref[...], SUBLN_EPS) * (1.0 - lam_init)).astype(BF16)


def _diff_attn(q, k, v, lam_params, subln_gain, lam_init, batch, seq, tq=256):
    T = batch * seq
    n_q = seq // tq
    width = 2 * HEAD_DIM
    return pl.pallas_call(
        functools.partial(_diff_attn_kernel, tq=tq, lam_init=lam_init),
        grid=(batch, DIFF_HEADS, n_q),
        in_specs=[
            pl.BlockSpec((tq, width), lambda b, h, i: (b * n_q + i, h)),
            pl.BlockSpec((seq, width), lambda b, h, i: (b, h)),
            pl.BlockSpec((seq, width), lambda b, h, i: (b, h)),
            pl.BlockSpec((4, HEAD_DIM), lambda b, h, i: (0, 0)),
            pl.BlockSpec((1, width), lambda b, h, i: (0, 0)),
        ],
        out_specs=pl.BlockSpec((tq, width), lambda b, h, i: (b * n_q + i, h)),
        out_shape=jax.ShapeDtypeStruct((T, D_MODEL), BF16),
        scratch_shapes=[pltpu.VMEM((2, tq, 1), F32), pltpu.VMEM((2, tq, 1), F32),
                        pltpu.VMEM((2, tq, width), F32)],
        compiler_params=_params(("parallel", "parallel", "arbitrary")),
        name="diff_attn",
    )(q, k, v, lam_params, subln_gain)


def _rope_tables(seq):
    inv_freq = ROPE_THETA ** (-jnp.arange(0, HEAD_DIM, 2, dtype=F32) / HEAD_DIM)
    ang = jnp.arange(seq, dtype=F32)[:, None] * inv_freq[None, :]
    cos, sin = jnp.cos(ang), jnp.sin(ang)
    return jnp.concatenate([cos, cos], axis=1), jnp.concatenate([-sin, sin], axis=1)


def kernel(x, norms, ffn_w_gate_up, ffn_w_down, dil_w_in, dil_w_out, diff_w_in, diff_w_out,
           diff_lambda, diff_subln):
    batch, seq, d_model = x.shape
    depth = norms.shape[0]
    cos, sin_signed = _rope_tables(seq)
    w_gate_up = ffn_w_gate_up.astype(BF16)
    w_down = ffn_w_down.astype(BF16)
    w_dil_in, w_dil_out = dil_w_in.astype(BF16), dil_w_out.astype(BF16)
    w_diff_in, w_diff_out = diff_w_in.astype(BF16), diff_w_out.astype(BF16)
    q_scale = HEAD_DIM ** -0.5
    gd = len(DILATIONS) * D_MODEL

    xt = x.reshape(batch * seq, d_model)
    for i in range(depth):
        gains = norms[i][:, None, :]
        j = i // N_MIXERS
        hid = _ffn_up(xt, gains[0], w_gate_up, i, 0)
        xt = _mm_norm_res(hid, w_down, (i, 0), xt, gains[1], 0.5, "ffn_down")
        if i % N_MIXERS == 0:
            segments = ([(g * D_MODEL, True, q_scale, (d,)) for g, d in enumerate(DILATIONS)]
                        + [(gd + g * D_MODEL, True, 1.0, (d,)) for g, d in enumerate(DILATIONS)]
                        + [(2 * gd, False, 1.0, DILATIONS)])
            q0, q1, q2, k0, k1, k2, v0, v1, v2 = _proj_in(
                xt, gains[2], w_dil_in, j, cos, sin_signed, segments, batch, seq)
            mixed = _dil_attn((q0, k0, v0, q1, k1, v1, q2, k2, v2), batch, seq)
            xt = _mm_norm_res(mixed, w_dil_out, (j,), xt, gains[3], 1.0, "dil_out")
        else:
            lam_init = 0.8 - 0.6 * math.exp(-0.3 * i)
            segments = [(0, True, q_scale, (1,)), (D_MODEL, True, 1.0, (1,)),
                        (2 * D_MODEL, False, 1.0, (1,))]
            q, k, v = _proj_in(xt, gains[2], w_diff_in, j, cos, sin_signed, segments, batch, seq)
            t = batch * seq
            mixed = _diff_attn(q.reshape(t, d_model), k.reshape(t, d_model), v.reshape(t, d_model),
                               diff_lambda[j], diff_subln[j][None, :], lam_init, batch, seq)
            xt = _mm_norm_res(mixed, w_diff_out, (j,), xt, gains[3], 1.0, "diff_out")
        hid = _ffn_up(xt, gains[4], w_gate_up, i, 1)
        xt = _mm_norm_res(hid, w_down, (i, 1), xt, gains[5], 0.5, "ffn_down")
    return xt.reshape(batch, seq, d_model)
```

```python
import functools
import math

import jax
import jax.numpy as jnp
from jax import lax
from jax.experimental import pallas as pl
from jax.experimental.pallas import tpu as pltpu

D_MODEL = 2048
HEAD_DIM = 128
DIL_GROUPS = ((128, 1), (512, 4), (2048, 16))
DILATIONS = tuple(d for _, d in DIL_GROUPS)
BAND = 128
DIL_HEADS = D_MODEL // HEAD_DIM
DIFF_HEADS = D_MODEL // (2 * HEAD_DIM)
D_FF = 5632
ROPE_THETA = 10000.0
NORM_EPS = 1e-6
SUBLN_EPS = 1e-5
N_MIXERS = 2

V7X_VMEM_LIMIT_BYTES = 56 * 1024 * 1024

F32 = jnp.float32
BF16 = jnp.bfloat16


def _params(semantics):
    return pltpu.CompilerParams(dimension_semantics=semantics,
                                vmem_limit_bytes=V7X_VMEM_LIMIT_BYTES)


def _rms(x, gain, eps):
    ms = jnp.mean(x * x, axis=-1, keepdims=True)
    return x * lax.rsqrt(ms + eps) * gain


def _ffn_up_kernel(x_ref, g_ref, wg_ref, wu_ref, o_ref, h_ref):
    @pl.when(pl.program_id(1) == 0)
    def _():
        h_ref[...] = _rms(x_ref[...], g_ref[...], NORM_EPS).astype(BF16)

    h = h_ref[...]
    g = jnp.dot(h, wg_ref[...], preferred_element_type=F32)
    u = jnp.dot(h, wu_ref[...], preferred_element_type=F32)
    o_ref[...] = (g * jax.nn.sigmoid(g) * u).astype(BF16)


def _ffn_up(x, gain, w_gate_up, layer, half, tm=1024, tn=512):
    T, D = x.shape
    n_tiles = D_FF // tn
    return pl.pallas_call(
        _ffn_up_kernel,
        grid=(T // tm, n_tiles),
        in_specs=[
            pl.BlockSpec((tm, D), lambda i, j: (i, 0)),
            pl.BlockSpec((1, D), lambda i, j: (0, 0)),
            pl.BlockSpec((None, None, D, tn), lambda i, j: (layer, half, 0, j)),
            pl.BlockSpec((None, None, D, tn), lambda i, j: (layer, half, 0, j + n_tiles)),
        ],
        out_specs=pl.BlockSpec((tm, tn), lambda i, j: (i, j)),
        out_shape=jax.ShapeDtypeStruct((T, D_FF), BF16),
        scratch_shapes=[pltpu.VMEM((tm, D), BF16)],
        compiler_params=_params(("parallel", "arbitrary")),
        name="ffn_up",
    )(x, gain, w_gate_up, w_gate_up)


def _mm_norm_res_kernel(a_ref, w_ref, x_ref, g_ref, o_ref, *, n_k, res_scale):
    k = pl.program_id(1)

    @pl.when(k == 0)
    def _():
        o_ref[...] = jnp.dot(a_ref[...], w_ref[...], preferred_element_type=F32)

    @pl.when(k > 0)
    def _():
        o_ref[...] += jnp.dot(a_ref[...], w_ref[...], preferred_element_type=F32)

    @pl.when(k == n_k - 1)
    def _():
        o_ref[...] = x_ref[...] + res_scale * _rms(o_ref[...], g_ref[...], NORM_EPS)


def _mm_norm_res(a, w, w_index, x, gain, res_scale, name, tm=1024, tk=512):
    T, K = a.shape
    D = x.shape[1]
    n_k = K // tk
    lead = (None,) * len(w_index)
    return pl.pallas_call(
        functools.partial(_mm_norm_res_kernel, n_k=n_k, res_scale=res_scale),
        grid=(T // tm, n_k),
        in_specs=[
            pl.BlockSpec((tm, tk), lambda i, k: (i, k)),
            pl.BlockSpec(lead + (tk, D), lambda i, k: w_index + (k, 0)),
            pl.BlockSpec((tm, D), lambda i, k: (i, 0)),
            pl.BlockSpec((1, D), lambda i, k: (0, 0)),
        ],
        out_specs=pl.BlockSpec((tm, D), lambda i, k: (i, 0)),
        out_shape=jax.ShapeDtypeStruct((T, D), F32),
        compiler_params=_params(("parallel", "arbitrary")),
        name=name,
    )(a, w, x, gain)


def _rope(y, cos, sin_signed):
    return y * cos + pltpu.roll(y, HEAD_DIM // 2, axis=1) * sin_signed


def _proj_in_kernel(*refs, segments, tn, seq_tiles):
    n_seg = len(segments)
    x_ref, g_ref, cos_ref, sin_ref = refs[:4]
    w_refs = refs[4:4 + n_seg]
    n_out = sum(len(seg[2]) for seg in segments)
    out_refs = refs[4 + n_seg:4 + n_seg + n_out]
    h_ref, y_ref = refs[4 + n_seg + n_out:]

    @pl.when(pl.program_id(1) == 0)
    def _():
        h_ref[...] = _rms(x_ref[...], g_ref[...], NORM_EPS).astype(BF16)

    h = h_ref[...]
    tm = h.shape[0]
    o = 0
    for s, (use_rope, scale, dils) in enumerate(segments):
        y = jnp.dot(h, w_refs[s][...], preferred_element_type=F32)
        heads = [y[:, c * HEAD_DIM:(c + 1) * HEAD_DIM] for c in range(tn // HEAD_DIM)]
        if use_rope:
            cos = cos_ref[...]
            sin = sin_ref[...]
            heads = [_rope(yh, cos, sin) * scale for yh in heads]
        if any(d > 1 for d in dils):
            for c, yh in enumerate(heads):
                y_ref[c] = yh
        for d in dils:
            out_ref = out_refs[o]
            o += 1
            for c, yh in enumerate(heads):
                cols = slice(c * HEAD_DIM, (c + 1) * HEAD_DIM)
                if d == 1:
                    out_ref[0, :, cols] = yh.astype(BF16)
                else:
                    for r in range(d):
                        out_ref[r, :, cols] = y_ref[c, pl.ds(r, tm // d, stride=d), :].astype(BF16)


def _proj_in(x, gain, w, w_layer, cos, sin_signed, segments, batch, seq, tm=1024, tn=256):
    T, D = x.shape
    seq_tiles = seq // tm
    n_col = D_MODEL // tn
    in_specs = [
        pl.BlockSpec((tm, D), lambda i, j: (i, 0)),
        pl.BlockSpec((1, D), lambda i, j: (0, 0)),
        pl.BlockSpec((tm, HEAD_DIM), lambda i, j: (i % seq_tiles, 0)),
        pl.BlockSpec((tm, HEAD_DIM), lambda i, j: (i % seq_tiles, 0)),
    ]
    out_specs, out_shapes = [], []
    for off, _, _, dils in segments:
        in_specs.append(pl.BlockSpec((None, D, tn),
                                     functools.partial(lambda i, j, c: (w_layer, 0, c + j), c=off // tn)))
        for d in dils:
            out_specs.append(pl.BlockSpec((None, d, tm // d, tn),
                                          lambda i, j: (i // seq_tiles, 0, i % seq_tiles, j)))
            out_shapes.append(jax.ShapeDtypeStruct((batch, d, seq // d, D_MODEL), BF16))
    kern = functools.partial(_proj_in_kernel, segments=tuple(s[1:] for s in segments), tn=tn,
                             seq_tiles=seq_tiles)
    return pl.pallas_call(
        kern,
        grid=(T // tm, n_col),
        in_specs=in_specs,
        out_specs=out_specs,
        out_shape=out_shapes,
        scratch_shapes=[pltpu.VMEM((tm, D), BF16), pltpu.VMEM((tn // HEAD_DIM, tm, HEAD_DIM), F32)],
        compiler_params=_params(("parallel", "arbitrary")),
        name="proj_in",
    )(x, gain, cos, sin_signed, *([w] * len(segments)))


def _band_block(q, k, v, delta):
    s = lax.dot_general(q, k, (((1,), (1,)), ((), ())), preferred_element_type=F32)
    back = delta + lax.broadcasted_iota(jnp.int32, s.shape, 0) - lax.broadcasted_iota(jnp.int32, s.shape, 1)
    s = jnp.where((back >= 0) & (back <= BAND), s, -jnp.inf)
    m = jnp.max(s, axis=-1, keepdims=True)
    p = jnp.exp(s - m)
    l = jnp.sum(p, axis=-1, keepdims=True)
    o = jnp.dot(p.astype(BF16), v, preferred_element_type=F32) / l
    return o, m + jnp.log(l)


def _dil_attn_kernel(q0, k0, v0, q1, k1, v1, q2, k2, v2, out_ref, o_scr, lse_scr, *, seq):
    groups = ((q0, k0, v0), (q1, k1, v1), (q2, k2, v2))
    for g, ((q_ref, k_ref, v_ref), d) in enumerate(zip(groups, DILATIONS)):
        n_blocks = seq // d // BAND
        for r in range(d):
            for n in range(n_blocks):
                k_start = max(n - 1, 0) * BAND
                q_start = n * BAND
                q = q_ref[r, q_start:q_start + BAND, :]
                k = k_ref[r, k_start:q_start + BAND, :]
                v = v_ref[r, k_start:q_start + BAND, :]
                o, lse = _band_block(q, k, v, q_start - k_start)
                rows = (pl.ds(q_start, BAND) if d == 1
                        else pl.ds(q_start * d + r, BAND, stride=d))
                o_scr[g, rows, :] = o
                lse_scr[g, rows, :] = jnp.broadcast_to(lse, (BAND, HEAD_DIM))

    l0, l1, l2 = lse_scr[0], lse_scr[1], lse_scr[2]
    m = jnp.maximum(jnp.maximum(l0, l1), l2)
    e0, e1, e2 = jnp.exp(l0 - m), jnp.exp(l1 - m), jnp.exp(l2 - m)
    den = e0 + e1 + e2
    out = (e0 / den) * o_scr[0] + (e1 / den) * o_scr[1] + (e2 / den) * o_scr[2]
    out_ref[...] = out.astype(BF16)


def _dil_attn(qkv, batch, seq):
    in_specs = []
    for d in DILATIONS:
        for _ in range(3):
            in_specs.append(pl.BlockSpec((None, d, seq // d, HEAD_DIM), lambda b, h: (b, 0, 0, h)))
    return pl.pallas_call(
        functools.partial(_dil_attn_kernel, seq=seq),
        grid=(batch, DIL_HEADS),
        in_specs=in_specs,
        out_specs=pl.BlockSpec((seq, HEAD_DIM), lambda b, h: (b, h)),
        out_shape=jax.ShapeDtypeStruct((batch * seq, D_MODEL), BF16),
        scratch_shapes=[pltpu.VMEM((3, seq, HEAD_DIM), F32), pltpu.VMEM((3, seq, HEAD_DIM), F32)],
        compiler_params=_params(("parallel", "parallel")),
        name="dil_attn",
    )(*qkv)


def _diff_attn_kernel(q_ref, k_ref, v_ref, lam_ref, gain_ref, o_ref, *, tq, seq, lam_init):
    width = 2 * HEAD_DIM
    reps = (1, width // HEAD_DIM)
    lp = lam_ref[...]
    lam = (jnp.exp(jnp.sum(lp[0:1] * lp[1:2], axis=-1, keepdims=True))
           - jnp.exp(jnp.sum(lp[2:3] * lp[3:4], axis=-1, keepdims=True)) + lam_init)
    row = lax.broadcasted_iota(jnp.int32, (tq, tq), 0)
    col = lax.broadcasted_iota(jnp.int32, (tq, tq), 1)

    for qi in range(seq // tq):
        q_rows = slice(qi * tq, (qi + 1) * tq)
        normed = []
        for c in range(2):
            lanes = slice(c * HEAD_DIM, (c + 1) * HEAD_DIM)
            q = q_ref[q_rows, lanes]
            m = l = acc = None
            for j in range(qi + 1):
                k_rows = slice(j * tq, (j + 1) * tq)
                s = lax.dot_general(q, k_ref[k_rows, lanes], (((1,), (1,)), ((), ())),
                                    preferred_element_type=F32)
                if j == qi:
                    s = jnp.where(row >= col, s, -jnp.inf)
                m_blk = jnp.max(s, axis=-1, keepdims=True)
                if j == 0:
                    m_new = jnp.broadcast_to(m_blk, (tq, HEAD_DIM))
                else:
                    m_new = jnp.maximum(m, m_blk)
                p = jnp.exp(s - jnp.tile(m_new, (1, tq // HEAD_DIM)))
                p_sum = jnp.sum(p, axis=-1, keepdims=True)
                pv = jnp.dot(p.astype(BF16), v_ref[k_rows, :], preferred_element_type=F32)
                if j == 0:
                    l = jnp.broadcast_to(p_sum, (tq, HEAD_DIM))
                    acc = pv
                else:
                    alpha = jnp.exp(m - m_new)
                    l = alpha * l + p_sum
                    acc = jnp.tile(alpha, reps) * acc + pv
                m = m_new
            normed.append(acc / jnp.tile(l, reps))
        o = normed[0] - lam * normed[1]
        o_ref[q_rows, :] = (_rms(o, gain_ref[...], SUBLN_EPS) * (1.0 - lam_init)).astype(BF16)


def _diff_attn(q, k, v, lam_params, subln_gain, lam_init, batch, seq, tq=512):
    T = batch * seq
    width = 2 * HEAD_DIM
    head_spec = pl.BlockSpec((seq, width), lambda b, h: (b, h))
    return pl.pallas_call(
        functools.partial(_diff_attn_kernel, tq=tq, seq=seq, lam_init=lam_init),
        grid=(batch, DIFF_HEADS),
        in_specs=[
            head_spec, head_spec, head_spec,
            pl.BlockSpec((4, HEAD_DIM), lambda b, h: (0, 0)),
            pl.BlockSpec((1, width), lambda b, h: (0, 0)),
        ],
        out_specs=head_spec,
        out_shape=jax.ShapeDtypeStruct((T, D_MODEL), BF16),
        compiler_params=_params(("parallel", "parallel")),
        name="diff_attn",
    )(q, k, v, lam_params, subln_gain)


def _rope_tables(seq):
    inv_freq = ROPE_THETA ** (-jnp.arange(0, HEAD_DIM, 2, dtype=F32) / HEAD_DIM)
    ang = jnp.arange(seq, dtype=F32)[:, None] * inv_freq[None, :]
    cos, sin = jnp.cos(ang), jnp.sin(ang)
    return jnp.concatenate([cos, cos], axis=1), jnp.concatenate([-sin, sin], axis=1)


def kernel(x, norms, ffn_w_gate_up, ffn_w_down, dil_w_in, dil_w_out, diff_w_in, diff_w_out,
           diff_lambda, diff_subln):
    batch, seq, d_model = x.shape
    depth = norms.shape[0]
    cos, sin_signed = _rope_tables(seq)
    w_gate_up = ffn_w_gate_up.astype(BF16)
    w_down = ffn_w_down.astype(BF16)
    w_dil_in, w_dil_out = dil_w_in.astype(BF16), dil_w_out.astype(BF16)
    w_diff_in, w_diff_out = diff_w_in.astype(BF16), diff_w_out.astype(BF16)
    q_scale = HEAD_DIM ** -0.5
    gd = len(DILATIONS) * D_MODEL

    xt = x.reshape(batch * seq, d_model)
    for i in range(depth):
        gains = norms[i][:, None, :]
        j = i // N_MIXERS
        hid = _ffn_up(xt, gains[0], w_gate_up, i, 0)
        xt = _mm_norm_res(hid, w_down, (i, 0), xt, gains[1], 0.5, "ffn_down")
        if i % N_MIXERS == 0:
            segments = ([(g * D_MODEL, True, q_scale, (d,)) for g, d in enumerate(DILATIONS)]
                        + [(gd + g * D_MODEL, True, 1.0, (d,)) for g, d in enumerate(DILATIONS)]
                        + [(2 * gd, False, 1.0, DILATIONS)])
            q0, q1, q2, k0, k1, k2, v0, v1, v2 = _proj_in(
                xt, gains[2], w_dil_in, j, cos, sin_signed, segments, batch, seq)
            mixed = _dil_attn((q0, k0, v0, q1, k1, v1, q2, k2, v2), batch, seq)
            xt = _mm_norm_res(mixed, w_dil_out, (j,), xt, gains[3], 1.0, "dil_out")
        else:
            lam_init = 0.8 - 0.6 * math.exp(-0.3 * i)
            segments = [(0, True, q_scale, (1,)), (D_MODEL, True, 1.0, (1,)),
                        (2 * D_MODEL, False, 1.0, (1,))]
            q, k, v = _proj_in(xt, gains[2], w_diff_in, j, cos, sin_signed, segments, batch, seq)
            t = batch * seq
            mixed = _diff_attn(q.reshape(t, d_model), k.reshape(t, d_model), v.reshape(t, d_model),
                               diff_lambda[j], diff_subln[j][None, :], lam_init, batch, seq)
            xt = _mm_norm_res(mixed, w_diff_out, (j,), xt, gains[3], 1.0, "diff_out")
        hid = _ffn_up(xt, gains[4], w_gate_up, i, 1)
        xt = _mm_norm_res(hid, w_down, (i, 1), xt, gains[5], 0.5, "ffn_down")
    return xt.reshape(batch, seq, d_model)
```

```python
import functools
import math

import jax
import jax.numpy as jnp
from jax import lax
from jax.experimental import pallas as pl
from jax.experimental.pallas import tpu as pltpu

D_MODEL = 2048
HEAD_DIM = 128
DIL_GROUPS = ((128, 1), (512, 4), (2048, 16))
DILATIONS = tuple(d for _, d in DIL_GROUPS)
BAND = 128
DIL_HEADS = D_MODEL // HEAD_DIM
DIFF_HEADS = D_MODEL // (2 * HEAD_DIM)
D_FF = 5632
ROPE_THETA = 10000.0
NORM_EPS = 1e-6
SUBLN_EPS = 1e-5
N_MIXERS = 2

V7X_VMEM_LIMIT_BYTES = 58 * 1024 * 1024

F32 = jnp.float32
BF16 = jnp.bfloat16


def _params(semantics):
    return pltpu.CompilerParams(dimension_semantics=semantics,
                                vmem_limit_bytes=V7X_VMEM_LIMIT_BYTES)


def _rms(x, gain, eps):
    ms = jnp.mean(x * x, axis=-1, keepdims=True)
    return x * lax.rsqrt(ms + eps) * gain


def _pre_norm_kernel(x_ref, g_ref, o_ref):
    o_ref[...] = _rms(x_ref[...], g_ref[...], NORM_EPS).astype(BF16)


def _pre_norm(x, gain, tm=512):
    T, D = x.shape
    return pl.pallas_call(
        _pre_norm_kernel,
        grid=(T // tm,),
        in_specs=[pl.BlockSpec((tm, D), lambda i: (i, 0)), pl.BlockSpec((1, D), lambda i: (0, 0))],
        out_specs=pl.BlockSpec((tm, D), lambda i: (i, 0)),
        out_shape=jax.ShapeDtypeStruct((T, D), BF16),
        compiler_params=_params(("parallel",)),
        name="pre_norm",
    )(x, gain)


def _ffn_up_kernel(h_ref, wg_ref, wu_ref, o_ref):
    h = h_ref[...]
    g = jnp.dot(h, wg_ref[...].astype(BF16), preferred_element_type=F32)
    u = jnp.dot(h, wu_ref[...].astype(BF16), preferred_element_type=F32)
    o_ref[...] = (g * jax.nn.sigmoid(g) * u).astype(BF16)


def _ffn_up(h, w_gate_up, layer, half, tm=1024, tn=512):
    T, D = h.shape
    n_tiles = D_FF // tn
    return pl.pallas_call(
        _ffn_up_kernel,
        grid=(n_tiles, T // tm),
        in_specs=[
            pl.BlockSpec((tm, D), lambda j, i: (i, 0)),
            pl.BlockSpec((None, None, D, tn), lambda j, i: (layer, half, 0, j)),
            pl.BlockSpec((None, None, D, tn), lambda j, i: (layer, half, 0, j + n_tiles)),
        ],
        out_specs=pl.BlockSpec((tm, tn), lambda j, i: (i, j)),
        out_shape=jax.ShapeDtypeStruct((T, D_FF), BF16),
        compiler_params=_params(("parallel", "parallel")),
        name="ffn_up",
    )(h, w_gate_up, w_gate_up)


def _mm_norm_res_kernel(*refs, n_load, tk, res_scale, with_next):
    if with_next:
        a_ref, w_ref, x_ref, g_ref, gn_ref, o_ref, hn_ref, w_scr = refs
    else:
        a_ref, w_ref, x_ref, g_ref, o_ref, w_scr = refs
    step = pl.program_id(0)

    @pl.when(step < n_load)
    def _():
        rows = pl.ds(pl.multiple_of(step * tk, tk), tk)
        w_scr[rows, :] = w_ref[...].astype(BF16)

    @pl.when(step >= n_load)
    def _():
        y = jnp.dot(a_ref[...], w_scr[...], preferred_element_type=F32)
        x_new = x_ref[...] + res_scale * _rms(y, g_ref[...], NORM_EPS)
        o_ref[...] = x_new
        if with_next:
            hn_ref[...] = _rms(x_new, gn_ref[...], NORM_EPS).astype(BF16)


def _mm_norm_res(a, w, w_index, x, gain, next_gain, res_scale, name, tm, tk=512):
    T, K = a.shape
    D = x.shape[1]
    n_load = K // tk
    lead = (None,) * len(w_index)
    with_next = next_gain is not None

    def row(s):
        return (jnp.maximum(s - n_load, 0), 0)

    in_specs = [
        pl.BlockSpec((tm, K), row),
        pl.BlockSpec(lead + (tk, D), lambda s: w_index + (jnp.minimum(s, n_load - 1), 0)),
        pl.BlockSpec((tm, D), row),
        pl.BlockSpec((1, D), lambda s: (0, 0)),
    ]
    args = [a, w, x, gain]
    out_specs = [pl.BlockSpec((tm, D), row)]
    out_shape = [jax.ShapeDtypeStruct((T, D), F32)]
    if with_next:
        in_specs.append(pl.BlockSpec((1, D), lambda s: (0, 0)))
        args.append(next_gain)
        out_specs.append(pl.BlockSpec((tm, D), row))
        out_shape.append(jax.ShapeDtypeStruct((T, D), BF16))
    outs = pl.pallas_call(
        functools.partial(_mm_norm_res_kernel, n_load=n_load, tk=tk, res_scale=res_scale,
                          with_next=with_next),
        grid=(n_load + T // tm,),
        in_specs=in_specs,
        out_specs=out_specs,
        out_shape=out_shape,
        scratch_shapes=[pltpu.VMEM((K, D), BF16)],
        compiler_params=_params(("arbitrary",)),
        name=name,
    )(*args)
    return (outs[0], outs[1]) if with_next else (outs[0], None)


def _rope(y, cos, sin_signed):
    return y * cos + pltpu.roll(y, HEAD_DIM // 2, axis=1) * sin_signed


def _proj_in_kernel(*refs, layouts, tn, tiles_per_range, q_scale):
    n_lay = len(layouts)
    h_ref, cos_ref, sin_ref = refs[:3]
    w_refs = refs[3:3 + n_lay]
    out_refs = refs[3 + n_lay:3 + 2 * n_lay]
    y_scr = refs[3 + 2 * n_lay]
    col_range = pl.program_id(0) // tiles_per_range
    h = h_ref[...]
    tm = h.shape[0]
    n_slabs = tn // HEAD_DIM

    def emit(lay, slabs):
        d = layouts[lay]
        out_ref = out_refs[lay]
        for c, slab in enumerate(slabs):
            cols = slice(c * HEAD_DIM, (c + 1) * HEAD_DIM)
            if d == 1:
                out_ref[0, :, cols] = slab.astype(BF16)
            else:
                y_scr[c] = slab
                for r in range(d):
                    out_ref[r, :, cols] = y_scr[c, pl.ds(r, tm // d, stride=d), :].astype(BF16)

    @pl.when(col_range < 2)
    def _():
        scale = jnp.where(col_range == 0, q_scale, 1.0).astype(F32)
        cos = cos_ref[...]
        sin = sin_ref[...]
        for lay in range(n_lay):
            y = jnp.dot(h, w_refs[lay][...].astype(BF16), preferred_element_type=F32)
            emit(lay, [_rope(y[:, c * HEAD_DIM:(c + 1) * HEAD_DIM], cos, sin) * scale
                       for c in range(n_slabs)])

    @pl.when(col_range == 2)
    def _():
        y = jnp.dot(h, w_refs[0][...].astype(BF16), preferred_element_type=F32)
        slabs = [y[:, c * HEAD_DIM:(c + 1) * HEAD_DIM] for c in range(n_slabs)]
        for lay in range(n_lay):
            emit(lay, slabs)


def _proj_in(h, w, w_layer, cos, sin_signed, layouts, col_offsets, batch, seq, tm=1024, tn=512):
    T, D = h.shape
    seq_tiles = seq // tm
    tiles_per_range = D_MODEL // tn

    def w_map(j, i, lay):
        q_t, k_t, v_t = (off // tn for off in col_offsets[lay])
        t = j % tiles_per_range
        v_tile = v_t + t if lay == 0 else k_t + tiles_per_range - 1
        rng = j // tiles_per_range
        return (w_layer, 0, jnp.where(rng == 0, q_t + t, jnp.where(rng == 1, k_t + t, v_tile)))

    in_specs = [
        pl.BlockSpec((tm, D), lambda j, i: (i, 0)),
        pl.BlockSpec((tm, HEAD_DIM), lambda j, i: (i % seq_tiles, 0)),
        pl.BlockSpec((tm, HEAD_DIM), lambda j, i: (i % seq_tiles, 0)),
    ]
    out_specs, out_shapes = [], []
    for lay, d in enumerate(layouts):
        in_specs.append(pl.BlockSpec((None, D, tn), functools.partial(w_map, lay=lay)))
        out_specs.append(pl.BlockSpec((None, d, tm // d, tn),
                                      lambda j, i: (i // seq_tiles, 0, i % seq_tiles, j)))
        out_shapes.append(jax.ShapeDtypeStruct((batch, d, seq // d, 3 * D_MODEL), BF16))
    kern = functools.partial(_proj_in_kernel, layouts=tuple(layouts), tn=tn,
                             tiles_per_range=tiles_per_range, q_scale=HEAD_DIM ** -0.5)
    return pl.pallas_call(
        kern,
        grid=(3 * tiles_per_range, T // tm),
        in_specs=in_specs,
        out_specs=out_specs,
        out_shape=out_shapes,
        scratch_shapes=[pltpu.VMEM((tn // HEAD_DIM, tm, HEAD_DIM), F32)],
        compiler_params=_params(("parallel", "arbitrary")),
        name="proj_in",
    )(h, cos, sin_signed, *([w] * len(layouts)))


def _band_block(q, k, v, delta):
    s = lax.dot_general(q, k, (((1,), (1,)), ((), ())), preferred_element_type=F32)
    back = delta + lax.broadcasted_iota(jnp.int32, s.shape, 0) - lax.broadcasted_iota(jnp.int32, s.shape, 1)
    s = jnp.where((back >= 0) & (back <= BAND), s, -jnp.inf)
    m = jnp.max(s, axis=-1, keepdims=True)
    p = jnp.exp(s - m)
    l = jnp.sum(p, axis=-1, keepdims=True)
    o = jnp.dot(p.astype(BF16), v, preferred_element_type=F32) / l
    return o, m + jnp.log(l)


def _dil_attn_kernel(q0, k0, v0, q1, k1, v1, q2, k2, v2, out_ref, o_scr, lse_scr, *, seq):
    groups = ((q0, k0, v0), (q1, k1, v1), (q2, k2, v2))
    for g, ((q_ref, k_ref, v_ref), d) in enumerate(zip(groups, DILATIONS)):
        n_blocks = seq // d // BAND
        for r in range(d):
            for n in range(n_blocks):
                k_start = max(n - 1, 0) * BAND
                q_start = n * BAND
                q = q_ref[r, q_start:q_start + BAND, :]
                k = k_ref[r, k_start:q_start + BAND, :]
                v = v_ref[r, k_start:q_start + BAND, :]
                o, lse = _band_block(q, k, v, q_start - k_start)
                rows = (pl.ds(q_start, BAND) if d == 1
                        else pl.ds(q_start * d + r, BAND, stride=d))
                o_scr[g, rows, :] = o
                lse_scr[g, rows, :] = jnp.broadcast_to(lse, (BAND, HEAD_DIM))

    l0, l1, l2 = lse_scr[0], lse_scr[1], lse_scr[2]
    m = jnp.maximum(jnp.maximum(l0, l1), l2)
    e0, e1, e2 = jnp.exp(l0 - m), jnp.exp(l1 - m), jnp.exp(l2 - m)
    den = e0 + e1 + e2
    out = (e0 / den) * o_scr[0] + (e1 / den) * o_scr[1] + (e2 / den) * o_scr[2]
    out_ref[...] = out.astype(BF16)


def _dil_attn(qkv_by_layout, batch, seq):
    in_specs, args = [], []
    for d, qkv in zip(DILATIONS, qkv_by_layout):
        for part in range(3):
            in_specs.append(pl.BlockSpec(
                (None, d, seq // d, HEAD_DIM),
                functools.partial(lambda b, h, part: (b, 0, 0, part * DIL_HEADS + h), part=part)))
            args.append(qkv)
    return pl.pallas_call(
        functools.partial(_dil_attn_kernel, seq=seq),
        grid=(batch, DIL_HEADS),
        in_specs=in_specs,
        out_specs=pl.BlockSpec((seq, HEAD_DIM), lambda b, h: (b, h)),
        out_shape=jax.ShapeDtypeStruct((batch * seq, D_MODEL), BF16),
        scratch_shapes=[pltpu.VMEM((3, seq, HEAD_DIM), F32), pltpu.VMEM((3, seq, HEAD_DIM), F32)],
        compiler_params=_params(("parallel", "parallel")),
        name="dil_attn",
    )(*args)


def _diff_attn_kernel(q_ref, k_ref, v_ref, lam_ref, gain_ref, o_ref, *, tq, seq, lam_init):
    width = 2 * HEAD_DIM
    reps = (1, width // HEAD_DIM)
    lp = lam_ref[...]
    lam = (jnp.exp(jnp.sum(lp[0:1] * lp[1:2], axis=-1, keepdims=True))
           - jnp.exp(jnp.sum(lp[2:3] * lp[3:4], axis=-1, keepdims=True)) + lam_init)
    row = lax.broadcasted_iota(jnp.int32, (tq, tq), 0)
    col = lax.broadcasted_iota(jnp.int32, (tq, tq), 1)

    for qi in range(seq // tq):
        q_rows = slice(qi * tq, (qi + 1) * tq)
        normed = []
        for c in range(2):
            lanes = slice(c * HEAD_DIM, (c + 1) * HEAD_DIM)
            q = q_ref[q_rows, lanes]
            m = l = acc = None
            for j in range(qi + 1):
                k_rows = slice(j * tq, (j + 1) * tq)
                s = lax.dot_general(q, k_ref[k_rows, lanes], (((1,), (1,)), ((), ())),
                                    preferred_element_type=F32)
                if j == qi:
                    s = jnp.where(row >= col, s, -jnp.inf)
                m_blk = jnp.max(s, axis=-1, keepdims=True)
                if j == 0:
                    m_new = jnp.broadcast_to(m_blk, (tq, HEAD_DIM))
                else:
                    m_new = jnp.maximum(m, m_blk)
                p = jnp.exp(s - jnp.tile(m_new, (1, tq // HEAD_DIM)))
                p_sum = jnp.sum(p, axis=-1, keepdims=True)
                pv = jnp.dot(p.astype(BF16), v_ref[k_rows, :], preferred_element_type=F32)
                if j == 0:
                    l = jnp.broadcast_to(p_sum, (tq, HEAD_DIM))
                    acc = pv
                else:
                    alpha = jnp.exp(m - m_new)
                    l = alpha * l + p_sum
                    acc = jnp.tile(alpha, reps) * acc + pv
                m = m_new
            normed.append(acc / jnp.tile(l, reps))
        o = normed[0] - lam * normed[1]
        o_ref[q_rows, :] = (_rms(o, gain_ref[...], SUBLN_EPS) * (1.0 - lam_init)).astype(BF16)


def _diff_attn(qkv, lam_params, subln_gain, lam_init, batch, seq, tq=512):
    T = batch * seq
    width = 2 * HEAD_DIM

    def part_spec(part):
        return pl.BlockSpec((seq, width), lambda b, h: (b, part * DIFF_HEADS + h))

    return pl.pallas_call(
        functools.partial(_diff_attn_kernel, tq=tq, seq=seq, lam_init=lam_init),
        grid=(batch, DIFF_HEADS),
        in_specs=[
            part_spec(0), part_spec(1), part_spec(2),
            pl.BlockSpec((4, HEAD_DIM), lambda b, h: (0, 0)),
            pl.BlockSpec((1, width), lambda b, h: (0, 0)),
        ],
        out_specs=pl.BlockSpec((seq, width), lambda b, h: (b, h)),
        out_shape=jax.ShapeDtypeStruct((T, D_MODEL), BF16),
        compiler_params=_params(("parallel", "parallel")),
        name="diff_attn",
    )(qkv, qkv, qkv, lam_params, subln_gain)


def _rope_tables(seq):
    inv_freq = ROPE_THETA ** (-jnp.arange(0, HEAD_DIM, 2, dtype=F32) / HEAD_DIM)
    ang = jnp.arange(seq, dtype=F32)[:, None] * inv_freq[None, :]
    cos, sin = jnp.cos(ang), jnp.sin(ang)
    return jnp.concatenate([cos, cos], axis=1), jnp.concatenate([-sin, sin], axis=1)


def kernel(x, norms, ffn_w_gate_up, ffn_w_down, dil_w_in, dil_w_out, diff_w_in, diff_w_out,
           diff_lambda, diff_subln):
    batch, seq, d_model = x.shape
    depth = norms.shape[0]
    t = batch * seq
    cos, sin_signed = _rope_tables(seq)
    gd = len(DILATIONS) * D_MODEL

    xt = x.reshape(t, d_model)
    h = _pre_norm(xt, norms[0, 0][None, :])
    for i in range(depth):
        gains = norms[i][:, None, :]
        j = i // N_MIXERS
        hid = _ffn_up(h, ffn_w_gate_up, i, 0)
        xt, h = _mm_norm_res(hid, ffn_w_down, (i, 0), xt, gains[1], gains[2], 0.5, "ffn_down", tm=256)
        if i % N_MIXERS == 0:
            offsets = [(g * D_MODEL, gd + g * D_MODEL, 2 * gd) for g in range(len(DILATIONS))]
            qkv = _proj_in(h, dil_w_in, j, cos, sin_signed, DILATIONS, offsets, batch, seq)
            mixed = _dil_attn(qkv, batch, seq)
            xt, h = _mm_norm_res(mixed, dil_w_out, (j,), xt, gains[3], gains[4], 1.0, "dil_out", tm=512)
        else:
            lam_init = 0.8 - 0.6 * math.exp(-0.3 * i)
            offsets = [(0, D_MODEL, 2 * D_MODEL)]
            (qkv,) = _proj_in(h, diff_w_in, j, cos, sin_signed, (1,), offsets, batch, seq)
            mixed = _diff_attn(qkv.reshape(t, 3 * d_model), diff_lambda[j], diff_subln[j][None, :],
                               lam_init, batch, seq)
            xt, h = _mm_norm_res(mixed, diff_w_out, (j,), xt, gains[3], gains[4], 1.0, "diff_out", tm=512)
        hid = _ffn_up(h, ffn_w_gate_up, i, 1)
        next_gain = norms[i + 1, 0][None, :] if i + 1 < depth else None
        xt, h = _mm_norm_res(hid, ffn_w_down, (i, 1), xt, gains[5], next_gain, 0.5, "ffn_down", tm=256)
    return xt.reshape(batch, seq, d_model)
```

```python
import functools
import math

import jax
import jax.numpy as jnp
from jax import lax
from jax.experimental import pallas as pl
from jax.experimental.pallas import tpu as pltpu

D_MODEL = 2048
HEAD_DIM = 128
DIL_GROUPS = ((128, 1), (512, 4), (2048, 16))
DILATIONS = tuple(d for _, d in DIL_GROUPS)
BAND = 128
DIL_HEADS = D_MODEL // HEAD_DIM
DIFF_HEADS = D_MODEL // (2 * HEAD_DIM)
D_FF = 5632
ROPE_THETA = 10000.0
NORM_EPS = 1e-6
SUBLN_EPS = 1e-5
N_MIXERS = 2
DIL_LOOKAHEAD = 8

V7X_VMEM_LIMIT_BYTES = 58 * 1024 * 1024

F32 = jnp.float32
BF16 = jnp.bfloat16


def _params(semantics, flags=None):
    return pltpu.CompilerParams(dimension_semantics=semantics,
                                vmem_limit_bytes=V7X_VMEM_LIMIT_BYTES, flags=flags)


def _rms(x, gain, eps):
    ms = jnp.mean(x * x, axis=-1, keepdims=True)
    return x * lax.rsqrt(ms + eps) * gain


def _pre_norm_kernel(x_ref, g_ref, o_ref):
    o_ref[...] = _rms(x_ref[...], g_ref[...], NORM_EPS).astype(BF16)


def _pre_norm(x, gain, tm=512):
    T, D = x.shape
    return pl.pallas_call(
        _pre_norm_kernel,
        grid=(T // tm,),
        in_specs=[pl.BlockSpec((tm, D), lambda i: (i, 0)), pl.BlockSpec((1, D), lambda i: (0, 0))],
        out_specs=pl.BlockSpec((tm, D), lambda i: (i, 0)),
        out_shape=jax.ShapeDtypeStruct((T, D), BF16),
        compiler_params=_params(("parallel",)),
        name="pre_norm",
    )(x, gain)


def _ffn_up_kernel(h_ref, wg_ref, wu_ref, o_ref):
    h = h_ref[...]
    g = jnp.dot(h, wg_ref[...].astype(BF16), preferred_element_type=F32)
    u = jnp.dot(h, wu_ref[...].astype(BF16), preferred_element_type=F32)
    o_ref[...] = (g * jax.nn.sigmoid(g) * u).astype(BF16)


def _ffn_up(h, w_gate_up, layer, half, tm=1024, tn=512):
    T, D = h.shape
    n_tiles = D_FF // tn
    return pl.pallas_call(
        _ffn_up_kernel,
        grid=(n_tiles, T // tm),
        in_specs=[
            pl.BlockSpec((tm, D), lambda j, i: (i, 0)),
            pl.BlockSpec((None, None, D, tn), lambda j, i: (layer, half, 0, j)),
            pl.BlockSpec((None, None, D, tn), lambda j, i: (layer, half, 0, j + n_tiles)),
        ],
        out_specs=pl.BlockSpec((tm, tn), lambda j, i: (i, j)),
        out_shape=jax.ShapeDtypeStruct((T, D_FF), BF16),
        compiler_params=_params(("parallel", "parallel")),
        name="ffn_up",
    )(h, w_gate_up, w_gate_up)


def _mm_norm_res_kernel(*refs, n_load, tk, res_scale, with_next):
    if with_next:
        a_ref, w_ref, x_ref, g_ref, gn_ref, o_ref, hn_ref, w_scr = refs
    else:
        a_ref, w_ref, x_ref, g_ref, o_ref, w_scr = refs
    step = pl.program_id(0)

    @pl.when(step < n_load)
    def _():
        rows = pl.ds(pl.multiple_of(step * tk, tk), tk)
        w_scr[rows, :] = w_ref[...].astype(BF16)

    @pl.when(step >= n_load)
    def _():
        y = jnp.dot(a_ref[...], w_scr[...], preferred_element_type=F32)
        x_new = x_ref[...] + res_scale * _rms(y, g_ref[...], NORM_EPS)
        o_ref[...] = x_new
        if with_next:
            hn_ref[...] = _rms(x_new, gn_ref[...], NORM_EPS).astype(BF16)


def _mm_norm_res(a, w, w_index, x, gain, next_gain, res_scale, name, tm, tk=512):
    T, K = a.shape
    D = x.shape[1]
    n_load = K // tk
    lead = (None,) * len(w_index)
    with_next = next_gain is not None

    def row(s):
        return (jnp.maximum(s - n_load, 0), 0)

    in_specs = [
        pl.BlockSpec((tm, K), row),
        pl.BlockSpec(lead + (tk, D), lambda s: w_index + (jnp.minimum(s, n_load - 1), 0)),
        pl.BlockSpec((tm, D), row),
        pl.BlockSpec((1, D), lambda s: (0, 0)),
    ]
    args = [a, w, x, gain]
    out_specs = [pl.BlockSpec((tm, D), row)]
    out_shape = [jax.ShapeDtypeStruct((T, D), F32)]
    if with_next:
        in_specs.append(pl.BlockSpec((1, D), lambda s: (0, 0)))
        args.append(next_gain)
        out_specs.append(pl.BlockSpec((tm, D), row))
        out_shape.append(jax.ShapeDtypeStruct((T, D), BF16))
    outs = pl.pallas_call(
        functools.partial(_mm_norm_res_kernel, n_load=n_load, tk=tk, res_scale=res_scale,
                          with_next=with_next),
        grid=(n_load + T // tm,),
        in_specs=in_specs,
        out_specs=out_specs,
        out_shape=out_shape,
        scratch_shapes=[pltpu.VMEM((K, D), BF16)],
        compiler_params=_params(("arbitrary",)),
        name=name,
    )(*args)
    return (outs[0], outs[1]) if with_next else (outs[0], None)


def _rope(y, cos, sin_signed):
    return y * cos + pltpu.roll(y, HEAD_DIM // 2, axis=1) * sin_signed


def _proj_in_kernel(*refs, layouts, tn, tiles_per_range, q_scale):
    n_lay = len(layouts)
    h_ref, cos_ref, sin_ref = refs[:3]
    w_refs = refs[3:3 + n_lay]
    out_refs = refs[3 + n_lay:3 + 2 * n_lay]
    y_scr = refs[3 + 2 * n_lay]
    col_range = pl.program_id(0) // tiles_per_range
    tm = h_ref.shape[0]
    n_slabs = tn // HEAD_DIM

    def emit(lay, slabs):
        d = layouts[lay]
        out_ref = out_refs[lay]
        for c, slab in enumerate(slabs):
            cols = slice(c * HEAD_DIM, (c + 1) * HEAD_DIM)
            if d == 1:
                out_ref[0, :, cols] = slab.astype(BF16)
            else:
                y_scr[c] = slab
                for r in range(d):
                    out_ref[r, :, cols] = y_scr[c, pl.ds(r, tm // d, stride=d), :].astype(BF16)

    @pl.when(col_range < 2)
    def _():
        scale = jnp.where(col_range == 0, q_scale, 1.0).astype(F32)
        cos = cos_ref[...]
        sin = sin_ref[...]
        for lay in range(n_lay):
            y = jnp.dot(h_ref[...], w_refs[lay][...].astype(BF16), preferred_element_type=F32)
            emit(lay, [_rope(y[:, c * HEAD_DIM:(c + 1) * HEAD_DIM], cos, sin) * scale
                       for c in range(n_slabs)])

    @pl.when(col_range == 2)
    def _():
        y = jnp.dot(h_ref[...], w_refs[0][...].astype(BF16), preferred_element_type=F32)
        slabs =[y[:, c * HEAD_DIM:(c + 1) * HEAD_DIM] for c in range(n_slabs)]
        for lay in range(n_lay):
            emit(lay, slabs)


def _proj_in(h, w, w_layer, cos, sin_signed, layouts, col_offsets, batch, seq, tm=1024, tn=512):
    T, D = h.shape
    seq_tiles = seq // tm
    tiles_per_range = D_MODEL // tn

    def w_map(j, i, lay):
        q_t, k_t, v_t = (off // tn for off in col_offsets[lay])
        t = j % tiles_per_range
        v_tile = v_t + t if lay == 0 else k_t + tiles_per_range - 1
        rng = j // tiles_per_range
        return (w_layer, 0, jnp.where(rng == 0, q_t + t, jnp.where(rng == 1, k_t + t, v_tile)))

    in_specs = [
        pl.BlockSpec((tm, D), lambda j, i: (i, 0)),
        pl.BlockSpec((tm, HEAD_DIM), lambda j, i: (i % seq_tiles, 0)),
        pl.BlockSpec((tm, HEAD_DIM), lambda j, i: (i % seq_tiles, 0)),
    ]
    out_specs, out_shapes = [], []
    for lay, d in enumerate(layouts):
        in_specs.append(pl.BlockSpec((None, D, tn), functools.partial(w_map, lay=lay)))
        out_specs.append(pl.BlockSpec((None, d, tm // d, tn),
                                      lambda j, i: (i // seq_tiles, 0, i % seq_tiles, j)))
        out_shapes.append(jax.ShapeDtypeStruct((batch, d, seq // d, 3 * D_MODEL), BF16))
    kern = functools.partial(_proj_in_kernel, layouts=tuple(layouts), tn=tn,
                             tiles_per_range=tiles_per_range, q_scale=HEAD_DIM ** -0.5)
    return pl.pallas_call(
        kern,
        grid=(3 * tiles_per_range, T // tm),
        in_specs=in_specs,
        out_specs=out_specs,
        out_shape=out_shapes,
        scratch_shapes=[pltpu.VMEM((tn // HEAD_DIM, tm, HEAD_DIM), F32)],
        compiler_params=_params(("parallel", "arbitrary")),
        name="proj_in",
    )(h, cos, sin_signed, *([w] * len(layouts)))


def _software_pipeline(n, lookahead, scores, finish):
    pending = {i: scores(i) for i in range(min(lookahead, n))}
    results = []
    for i in range(n):
        if i + lookahead < n:
            pending[i + lookahead] = scores(i + lookahead)
        results.append(finish(i, pending.pop(i)))
    return results


def _band_scores(q, k, delta):
    s = lax.dot_general(q, k, (((1,), (1,)), ((), ())), preferred_element_type=F32)
    back = delta + lax.broadcasted_iota(jnp.int32, s.shape, 0) - lax.broadcasted_iota(jnp.int32, s.shape, 1)
    return jnp.where((back >= 0) & (back <= BAND), s, -jnp.inf)


def _dil_attn_kernel(q0, k0, v0, q1, k1, v1, q2, k2, v2, out_ref, o_scr, lse_scr, *, seq):
    groups = ((q0, k0, v0), (q1, k1, v1), (q2, k2, v2))
    blocks = [(g, r, n) for g, d in enumerate(DILATIONS)
              for r in range(d) for n in range(seq // d // BAND)]

    def key_rows(n):
        return slice(max(n - 1, 0) * BAND, (n + 1) * BAND)

    def scores(i):
        g, r, n = blocks[i]
        q_ref, k_ref, _ = groups[g]
        return _band_scores(q_ref[r, n * BAND:(n + 1) * BAND, :], k_ref[r, key_rows(n), :],
                            n * BAND - key_rows(n).start)

    def finish(i, s):
        g, r, n = blocks[i]
        d = DILATIONS[g]
        m = jnp.max(s, axis=-1, keepdims=True)
        p = jnp.exp(s - m)
        l = jnp.sum(p, axis=-1, keepdims=True)
        o = jnp.dot(p.astype(BF16), groups[g][2][r, key_rows(n), :], preferred_element_type=F32)
        rows = pl.ds(n * BAND, BAND) if d == 1 else pl.ds(n * BAND * d + r, BAND, stride=d)
        o_scr[g, rows, :] = o / l
        lse_scr[g, rows, :] = jnp.broadcast_to(m + jnp.log(l), (BAND, HEAD_DIM))

    _software_pipeline(len(blocks), DIL_LOOKAHEAD, scores, finish)

    l0, l1, l2 = lse_scr[0], lse_scr[1], lse_scr[2]
    m = jnp.maximum(jnp.maximum(l0, l1), l2)
    e0, e1, e2 = jnp.exp(l0 - m), jnp.exp(l1 - m), jnp.exp(l2 - m)
    den = e0 + e1 + e2
    out = (e0 / den) * o_scr[0] + (e1 / den) * o_scr[1] + (e2 / den) * o_scr[2]
    out_ref[...] = out.astype(BF16)


def _dil_attn(qkv_by_layout, batch, seq):
    in_specs, args = [], []
    for d, qkv in zip(DILATIONS, qkv_by_layout):
        for part in range(3):
            in_specs.append(pl.BlockSpec(
                (None, d, seq // d, HEAD_DIM),
                functools.partial(lambda b, h, part: (b, 0, 0, part * DIL_HEADS + h), part=part)))
            args.append(qkv)
    return pl.pallas_call(
        functools.partial(_dil_attn_kernel, seq=seq),
        grid=(batch, DIL_HEADS),
        in_specs=in_specs,
        out_specs=pl.BlockSpec((seq, HEAD_DIM), lambda b, h: (b, h)),
        out_shape=jax.ShapeDtypeStruct((batch * seq, D_MODEL), BF16),
        scratch_shapes=[pltpu.VMEM((3, seq, HEAD_DIM), F32), pltpu.VMEM((3, seq, HEAD_DIM), F32)],
        compiler_params=_params(("parallel", "parallel")),
        name="dil_attn",
    )(*args)


def _diff_attn_kernel(q_ref, k_ref, v_ref, lam_ref, gain_ref, o_ref, *, tq, seq, lam_init):
    lp = lam_ref[...]
    lam = (jnp.exp(jnp.sum(lp[0:1] * lp[1:2], axis=-1, keepdims=True))
           - jnp.exp(jnp.sum(lp[2:3] * lp[3:4], axis=-1, keepdims=True)) + lam_init)
    row = lax.broadcasted_iota(jnp.int32, (tq, tq), 0)
    col = lax.broadcasted_iota(jnp.int32, (tq, tq), 1)
    pairs = [(qi, c) for qi in reversed(range(seq // tq)) for c in range(2)]

    def scores(i):
        qi, c = pairs[i]
        lanes = slice(c * HEAD_DIM, (c + 1) * HEAD_DIM)
        s = lax.dot_general(q_ref[qi * tq:(qi + 1) * tq, lanes], k_ref[0:(qi + 1) * tq, lanes],
                            (((1,), (1,)), ((), ())), preferred_element_type=F32)
        diag = jnp.where(row >= col, s[:, qi * tq:], -jnp.inf)
        return diag if qi == 0 else jnp.concatenate([s[:, :qi * tq], diag], axis=1)

    def finish(i, s):
        qi, c = pairs[i]
        p = jnp.exp(s - jnp.max(s, axis=-1, keepdims=True))
        l = jnp.sum(p, axis=-1, keepdims=True)
        return jnp.dot(p.astype(BF16), v_ref[0:(qi + 1) * tq, :], preferred_element_type=F32) / l

    normed = dict(zip(pairs, _software_pipeline(len(pairs), 1, scores, finish)))
    for qi in range(seq // tq):
        o = normed[qi, 0] - lam * normed[qi, 1]
        o_ref[qi * tq:(qi + 1) * tq, :] = (_rms(o, gain_ref[...], SUBLN_EPS)
                                          * (1.0 - lam_init)).astype(BF16)


def _diff_attn(qkv, lam_params, subln_gain, lam_init, batch, seq, tq=512):
    T = batch * seq
    width = 2 * HEAD_DIM

    def part_spec(part):
        return pl.BlockSpec((seq, width), lambda b, h: (b, part * DIFF_HEADS + h))

    return pl.pallas_call(
        functools.partial(_diff_attn_kernel, tq=tq, seq=seq, lam_init=lam_init),
        grid=(batch, DIFF_HEADS),
        in_specs=[
            part_spec(0), part_spec(1), part_spec(2),
            pl.BlockSpec((4, HEAD_DIM), lambda b, h: (0, 0)),
            pl.BlockSpec((1, width), lambda b, h: (0, 0)),
        ],
        out_specs=pl.BlockSpec((seq, width), lambda b, h: (b, h)),
        out_shape=jax.ShapeDtypeStruct((T, D_MODEL), BF16),
        compiler_params=_params(("parallel", "parallel")),
        name="diff_attn",
    )(qkv, qkv, qkv, lam_params, subln_gain)


def _rope_tables(seq):
    inv_freq = ROPE_THETA ** (-jnp.arange(0, HEAD_DIM, 2, dtype=F32) / HEAD_DIM)
    ang = jnp.arange(seq, dtype=F32)[:, None] * inv_freq[None, :]
    cos, sin = jnp.cos(ang), jnp.sin(ang)
    return jnp.concatenate([cos, cos], axis=1), jnp.concatenate([-sin, sin], axis=1)


def kernel(x, norms, ffn_w_gate_up, ffn_w_down, dil_w_in, dil_w_out, diff_w_in, diff_w_out,
           diff_lambda, diff_subln):
    batch, seq, d_model = x.shape
    depth = norms.shape[0]
    t = batch * seq
    cos, sin_signed = _rope_tables(seq)
    gd = len(DILATIONS) * D_MODEL

    xt = x.reshape(t, d_model)
    h = _pre_norm(xt, norms[0, 0][None, :])
    for i in range(depth):
        gains = norms[i][:, None, :]
        j = i // N_MIXERS
        hid = _ffn_up(h, ffn_w_gate_up, i, 0)
        xt, h = _mm_norm_res(hid, ffn_w_down, (i, 0), xt, gains[1], gains[2], 0.5, "ffn_down", tm=256)
        if i % N_MIXERS == 0:
            offsets = [(g * D_MODEL, gd + g * D_MODEL, 2 * gd) for g in range(len(DILATIONS))]
            qkv = _proj_in(h, dil_w_in, j, cos, sin_signed, DILATIONS, offsets, batch, seq)
            mixed = _dil_attn(qkv, batch, seq)
            xt, h = _mm_norm_res(mixed, dil_w_out, (j,), xt, gains[3], gains[4], 1.0, "dil_out", tm=512)
        else:
            lam_init = 0.8 - 0.6 * math.exp(-0.3 * i)
            offsets = [(0, D_MODEL, 2 * D_MODEL)]
            (qkv,) = _proj_in(h, diff_w_in, j, cos, sin_signed, (1,), offsets, batch, seq)
            mixed = _diff_attn(qkv.reshape(t, 3 * d_model), diff_lambda[j], diff_subln[j][None, :],
                               lam_init, batch, seq)
            xt, h = _mm_norm_res(mixed, diff_w_out, (j,), xt, gains[3], gains[4], 1.0, "diff_out", tm=512)
        hid = _ffn_up(h, ffn_w_gate_up, i, 1)
        next_gain = norms[i + 1, 0][None, :] if i + 1 < depth else None
        xt, h = _mm_norm_res(hid, ffn_w_down, (i, 1), xt, gains[5], next_gain, 0.5, "ffn_down", tm=256)
    return xt.reshape(batch, seq, d_model)
```
